```python
import math, functools
import jax, jax.numpy as jnp
from jax import lax
import numpy as np

D_MODEL = 1024
BATCH = 8
SEQ = 2048
DEPTH = 4
DEC_BATCH = 32
DEC_SEQ = 4
PAST_LEN = 8192
PAGE_SIZE = 128

N_A_LAYERS = DEPTH // 2
N_B_LAYERS = DEPTH - N_A_LAYERS
D_CONV = D_MODEL
CONV_A_WIDTH = 31
HEAD_DIM = 64
N_HEADS = D_MODEL // HEAD_DIM
D_FF = ((8 * D_MODEL // 3 + 127) // 128) * 128
FFN_CONV_WIDTH = 3
Q_BLOCK = 128
SB_BIAS_INIT = -6.0
EPS = 1e-6

kernel_name = "yoco_conformer_stickbreaking_step"


def rms_norm(x, g):
    xf = x.astype(jnp.float32)
    y = xf * lax.rsqrt(jnp.mean(xf * xf, axis=-1, keepdims=True) + EPS) * g.astype(jnp.float32)
    return y.astype(x.dtype)


def layer_norm(x, g, b):
    xf = x.astype(jnp.float32)
    mu = jnp.mean(xf, axis=-1, keepdims=True)
    xc = xf - mu
    var = jnp.mean(xc * xc, axis=-1, keepdims=True)
    y = xc * lax.rsqrt(var + EPS) * g.astype(jnp.float32) + b.astype(jnp.float32)
    return y.astype(x.dtype)


def depthwise_causal_conv(x_ext, w):
    c = x_ext.shape[-1]
    return lax.conv_general_dilated(
        x_ext, w[:, None, :].astype(x_ext.dtype), window_strides=(1,), padding='VALID',
        dimension_numbers=('NWC', 'WIO', 'NWC'), feature_group_count=c)


def conformer_conv_module(x, prev, w_glu, b_glu, w_dw, b_dw, ln_g, ln_b, w_out, b_out):
    a, g = jnp.split(x @ w_glu + b_glu, 2, axis=-1)
    u = a * jax.nn.sigmoid(g)
    u_ext = jnp.concatenate([prev.astype(u.dtype), u], axis=1)
    c = depthwise_causal_conv(u_ext, w_dw) + b_dw
    c = jax.nn.silu(layer_norm(c, ln_g, ln_b))
    return c @ w_out + b_out, u_ext[:, -(CONV_A_WIDTH - 1):]


def conv_ffn(x, prev, w_gate, w_up, w_dw, w_down):
    g = x @ w_gate
    g_ext = jnp.concatenate([prev.astype(g.dtype), g], axis=1)
    g_c = depthwise_causal_conv(g_ext, w_dw)
    y = (jax.nn.gelu(g_c, approximate=True) * (x @ w_up)) @ w_down
    return y, g_ext[:, -(FFN_CONV_WIDTH - 1):]


def stick_breaking_attention(q, k, v, bias, q_offset):
    tq = q.shape[1]
    scale = q.shape[-1] ** -0.5
    bias_f = bias.astype(jnp.float32)[None, :, None, None]
    outs = []
    for start in range(0, tq, Q_BLOCK):
        stop = min(start + Q_BLOCK, tq)
        n_keys = q_offset + stop
        qb = q[:, start:stop].astype(jnp.float32)
        kb = k[:, :n_keys].astype(jnp.float32)
        vb = v[:, :n_keys].astype(jnp.float32)
        z = jnp.einsum('bqhd,bkhd->bhqk', qb, kb) * scale + bias_f
        t_pos = q_offset + start + jnp.arange(stop - start)
        s_pos = jnp.arange(n_keys)
        causal = s_pos[None, :] < t_pos[:, None]
        log_one_minus = jnp.where(causal, jax.nn.log_sigmoid(-z), 0.0)
        after = lax.cumsum(log_one_minus, axis=3, reverse=True) - log_one_minus
        w = jnp.where(causal, jnp.exp(jax.nn.log_sigmoid(z) + after), 0.0)
        o = jnp.einsum('bhqk,bkhd->bqhd', w, vb)
        outs.append(o.astype(q.dtype))
    return jnp.concatenate(outs, axis=1)


def trunk(x, conv_prev, ffn_prev, past_k, past_v, p):
    b, t, _ = x.shape
    h = x
    new_conv, new_ffn = [], []
    k_new = v_new = keys = values = None
    for layer in range(DEPTH):
        hn = rms_norm(h, p['norm_mix_pre'][layer])
        if layer < N_A_LAYERS:
            m, st = conformer_conv_module(
                hn, conv_prev[layer], p['conv_w_glu'][layer], p['conv_b_glu'][layer],
                p['conv_w_dw'][layer], p['conv_b_dw'][layer], p['conv_ln_g'][layer],
                p['conv_ln_b'][layer], p['conv_w_out'][layer], p['conv_b_out'][layer])
            new_conv.append(st)
        else:
            if layer == N_A_LAYERS:
                kv_in = rms_norm(h, p['kv_norm'])
                k_new = (kv_in @ p['w_k']).reshape(b, t, N_HEADS, HEAD_DIM)
                v_new = (kv_in @ p['w_v']).reshape(b, t, N_HEADS, HEAD_DIM)
                keys = jnp.concatenate([past_k.astype(k_new.dtype), k_new], axis=1)
                values = jnp.concatenate([past_v.astype(v_new.dtype), v_new], axis=1)
            j = layer - N_A_LAYERS
            q = (hn @ p['w_q'][j]).reshape(b, t, N_HEADS, HEAD_DIM)
            o = stick_breaking_attention(q, keys, values, p['sb_bias'][j], past_k.shape[1])
            m = o.reshape(b, t, N_HEADS * HEAD_DIM) @ p['w_o'][j]
        h = h + rms_norm(m, p['norm_mix_post'][layer])
        f, st_f = conv_ffn(rms_norm(h, p['norm_ffn_pre'][layer]), ffn_prev[layer],
                           p['ffn_w_gate'][layer], p['ffn_w_up'][layer],
                           p['ffn_w_dw'][layer], p['ffn_w_down'][layer])
        new_ffn.append(st_f)
        h = h + rms_norm(f, p['norm_ffn_post'][layer])
    return h, jnp.stack(new_conv), jnp.stack(new_ffn), k_new, v_new


def setup_inputs(seed: int = 0) -> dict:
    key = jax.random.key(seed)
    ks = iter(jax.random.split(key, 40))
    f32 = jnp.float32
    n_pages = PAST_LEN // PAGE_SIZE
    n_used = DEC_BATCH * n_pages
    n_pool = n_used + max(n_used // 4, 1)
    hd = N_HEADS * HEAD_DIM

    def nrm(shape, scale):
        return jax.random.normal(next(ks), shape, f32) * scale

    def gain(shape):
        return 1.0 + nrm(shape, 0.02)

    x_prompt = nrm((BATCH, SEQ, D_MODEL), 1.0)
    x_sample = nrm((DEC_BATCH, DEC_SEQ, D_MODEL), 1.0)
    state_conv_a = nrm((N_A_LAYERS, DEC_BATCH, CONV_A_WIDTH - 1, D_CONV), 0.5)
    state_ffn_conv = nrm((DEPTH, DEC_BATCH, FFN_CONV_WIDTH - 1, D_FF), 1.0)
    cache_k = nrm((n_pool, PAGE_SIZE, N_HEADS, HEAD_DIM), 1.0)
    cache_v = nrm((n_pool, PAGE_SIZE, N_HEADS, HEAD_DIM), 1.0)
    perm = jax.random.permutation(next(ks), n_pool)
    page_table = perm[:n_used].reshape(DEC_BATCH, n_pages).astype(jnp.int32)
    return {
        'x_prompt': x_prompt,
        'x_sample': x_sample,
        'state_conv_a': state_conv_a,
        'state_ffn_conv': state_ffn_conv,
        'cache_k': cache_k,
        'cache_v': cache_v,
        'page_table': page_table,
        'norm_mix_pre': gain((DEPTH, D_MODEL)),
        'norm_mix_post': gain((DEPTH, D_MODEL)),
        'norm_ffn_pre': gain((DEPTH, D_MODEL)),
        'norm_ffn_post': gain((DEPTH, D_MODEL)),
        'conv_w_glu': nrm((N_A_LAYERS, D_MODEL, 2 * D_CONV), D_MODEL ** -0.5),
        'conv_b_glu': nrm((N_A_LAYERS, 2 * D_CONV), 0.02),
        'conv_w_dw': nrm((N_A_LAYERS, CONV_A_WIDTH, D_CONV), CONV_A_WIDTH ** -0.5),
        'conv_b_dw': nrm((N_A_LAYERS, D_CONV), 0.02),
        'conv_ln_g': gain((N_A_LAYERS, D_CONV)),
        'conv_ln_b': nrm((N_A_LAYERS, D_CONV), 0.02),
        'conv_w_out': nrm((N_A_LAYERS, D_CONV, D_MODEL), D_CONV ** -0.5),
        'conv_b_out': nrm((N_A_LAYERS, D_MODEL), 0.02),
        'kv_norm': gain((D_MODEL,)),
        'w_k': nrm((D_MODEL, hd), D_MODEL ** -0.5),
        'w_v': nrm((D_MODEL, hd), D_MODEL ** -0.5),
        'w_q': nrm((N_B_LAYERS, D_MODEL, hd), D_MODEL ** -0.5),
        'w_o': nrm((N_B_LAYERS, hd, D_MODEL), hd ** -0.5),
        'sb_bias': SB_BIAS_INIT + nrm((N_B_LAYERS, N_HEADS), 0.1),
        'ffn_w_gate': nrm((DEPTH, D_MODEL, D_FF), D_MODEL ** -0.5),
        'ffn_w_up': nrm((DEPTH, D_MODEL, D_FF), D_MODEL ** -0.5),
        'ffn_w_dw': nrm((DEPTH, FFN_CONV_WIDTH, D_FF), FFN_CONV_WIDTH ** -0.5),
        'ffn_w_down': nrm((DEPTH, D_FF, D_MODEL), D_FF ** -0.5),
    }


def reference(x_prompt, x_sample, state_conv_a, state_ffn_conv, cache_k, cache_v, page_table,
              norm_mix_pre, norm_mix_post, norm_ffn_pre, norm_ffn_post,
              conv_w_glu, conv_b_glu, conv_w_dw, conv_b_dw, conv_ln_g, conv_ln_b,
              conv_w_out, conv_b_out, kv_norm, w_k, w_v, w_q, w_o, sb_bias,
              ffn_w_gate, ffn_w_up, ffn_w_dw, ffn_w_down):
    p = dict(norm_mix_pre=norm_mix_pre, norm_mix_post=norm_mix_post,
             norm_ffn_pre=norm_ffn_pre, norm_ffn_post=norm_ffn_post,
             conv_w_glu=conv_w_glu, conv_b_glu=conv_b_glu, conv_w_dw=conv_w_dw,
             conv_b_dw=conv_b_dw, conv_ln_g=conv_ln_g, conv_ln_b=conv_ln_b,
             conv_w_out=conv_w_out, conv_b_out=conv_b_out, kv_norm=kv_norm,
             w_k=w_k, w_v=w_v, w_q=w_q, w_o=w_o, sb_bias=sb_bias, ffn_w_gate=ffn_w_gate,
             ffn_w_up=ffn_w_up, ffn_w_dw=ffn_w_dw, ffn_w_down=ffn_w_down)

    bp = x_prompt.shape[0]
    dt = x_prompt.dtype
    conv_prev_p = jnp.zeros((N_A_LAYERS, bp, CONV_A_WIDTH - 1, D_CONV), dt)
    ffn_prev_p = jnp.zeros((DEPTH, bp, FFN_CONV_WIDTH - 1, D_FF), dt)
    empty_kv = jnp.zeros((bp, 0, N_HEADS, HEAD_DIM), dt)
    y_prompt, conv_new_p, ffn_new_p, k_new_p, v_new_p = trunk(
        x_prompt, conv_prev_p, ffn_prev_p, empty_kv, empty_kv, p)

    bs = x_sample.shape[0]
    past_k = cache_k[page_table].reshape(bs, -1, N_HEADS, HEAD_DIM)
    past_v = cache_v[page_table].reshape(bs, -1, N_HEADS, HEAD_DIM)
    y_sample, conv_new_s, ffn_new_s, k_new_s, v_new_s = trunk(
        x_sample, state_conv_a, state_ffn_conv, past_k, past_v, p)

    return (y_prompt, y_sample, conv_new_p, conv_new_s, ffn_new_p, ffn_new_s,
            k_new_p, v_new_p, k_new_s, v_new_s)
```

```python
import functools

import jax
import jax.numpy as jnp
from jax import lax
from jax.experimental import pallas as pl
from jax.experimental.pallas import tpu as pltpu

F32 = jnp.float32
BF16 = jnp.bfloat16

EPS = 1e-6
HEAD_DIM = 64
HEADS_PER_LANE_TILE = 2
LANES = 128
SUBLANES = 8
VMEM_LIMIT_BYTES = 56 * 1024 * 1024
ROW_BLOCK = 64
COL_CHUNK = 256
PROMPT_TILE = 512
ATTN_TQ = 256
ATTN_TK = 256
PAGES_PER_STEP = 8


def _round_up(x, m):
    return (x + m - 1) // m * m


def _const_spec(shape):
    nd = len(shape)
    return pl.BlockSpec(shape, lambda *_: (0,) * nd, pipeline_mode=pl.Buffered(1))


def _rows(r, n):
    return pl.ds(pl.multiple_of(r * n, n), n)


def _rms_rows(x, gain):
    return x * lax.rsqrt(jnp.mean(x * x, axis=-1, keepdims=True) + EPS) * gain


def _norm_to_bf16(h_ref, gain_ref, xs_ref, tm):
    def body(r, _):
        rows = _rows(r, ROW_BLOCK)
        xs_ref[rows, :] = _rms_rows(h_ref[rows, :], gain_ref[...]).astype(BF16)
        return 0
    lax.fori_loop(0, tm // ROW_BLOCK, body, 0)


def _residual_post_norm(h_ref, m_ref, gain_ref, out_ref, tm):
    def body(r, _):
        rows = _rows(r, ROW_BLOCK)
        out_ref[rows, :] = h_ref[rows, :] + _rms_rows(m_ref[rows, :], gain_ref[...])
        return 0
    lax.fori_loop(0, tm // ROW_BLOCK, body, 0)


def _mixer_a_kernel(*refs, tm, stride, halo, taps, tiles_per_seq, has_prev):
    if has_prev:
        (h_ref, prev_ref, gpre_ref, wglu_ref, bglu_ref, wdw_ref, bdw_ref, lng_ref, lnb_ref,
         wout_ref, bout_ref, gpost_ref, hout_ref, state_ref,
         xs_ref, ext_ref, cv_ref, cs_ref) = refs
    else:
        (h_ref, gpre_ref, wglu_ref, bglu_ref, wdw_ref, bdw_ref, lng_ref, lnb_ref,
         wout_ref, bout_ref, gpost_ref, hout_ref, state_ref,
         xs_ref, ext_ref, cv_ref, cs_ref) = refs
    d = h_ref.shape[1]
    hist = (taps - 1) * stride
    first = halo - hist

    if has_prev:
        ext_ref[first:halo, :] = prev_ref[...]
    else:
        @pl.when(pl.program_id(0) % tiles_per_seq == 0)
        def _():
            ext_ref[0:halo, :] = jnp.zeros((halo, d), F32)

    _norm_to_bf16(h_ref, gpre_ref, xs_ref, tm)

    for c in range(d // COL_CHUNK):
        lo, hi = c * COL_CHUNK, (c + 1) * COL_CHUNK
        a = jnp.dot(xs_ref[...], wglu_ref[:, lo:hi], preferred_element_type=F32) + bglu_ref[:, lo:hi]
        g = jnp.dot(xs_ref[...], wglu_ref[:, d + lo:d + hi],
                    preferred_element_type=F32) + bglu_ref[:, d + lo:d + hi]
        ext_ref[halo:halo + tm, lo:hi] = a * jax.nn.sigmoid(g)

    rb = min(ROW_BLOCK, tm)
    for c in range(d // COL_CHUNK):
        lo, hi = c * COL_CHUNK, (c + 1) * COL_CHUNK

        def conv_body(r, _, lo=lo, hi=hi):
            base = pl.multiple_of(r * rb, rb)
            acc = jnp.broadcast_to(bdw_ref[:, lo:hi], (rb, COL_CHUNK))
            if stride % SUBLANES == 0:
                for k in range(taps):
                    acc = acc + wdw_ref[k:k + 1, lo:hi] * ext_ref[pl.ds(base + first + k * stride, rb), lo:hi]
            else:
                win = ext_ref[pl.ds(base, rb + halo), lo:hi]
                for res in range(SUBLANES):
                    offs = [first + k for k in range(taps) if (first + k) % SUBLANES == res]
                    if not offs:
                        continue
                    span = max(offs) - res + rb
                    shifted = win[res:res + span] if res else win
                    for o in offs:
                        a0 = o - res
                        acc = acc + wdw_ref[o - first:o - first + 1, lo:hi] * shifted[a0:a0 + rb]
            cv_ref[pl.ds(base, rb), lo:hi] = acc
            return 0
        lax.fori_loop(0, tm // rb, conv_body, 0)

    if has_prev:
        state_ref[...] = ext_ref[halo + tm - hist:halo + tm, :]
    else:
        state_ref[0] = ext_ref[halo + tm - hist:halo + tm, :]
        ext_ref[0:halo, :] = ext_ref[tm:tm + halo, :]

    def ln_body(r, _):
        rows = _rows(r, ROW_BLOCK)
        cvals = cv_ref[rows, :]
        mu = jnp.mean(cvals, axis=-1, keepdims=True)
        xc = cvals - mu
        var = jnp.mean(xc * xc, axis=-1, keepdims=True)
        y = xc * lax.rsqrt(var + EPS) * lng_ref[...] + lnb_ref[...]
        cs_ref[rows, :] = (y * jax.nn.sigmoid(y)).astype(BF16)
        return 0
    lax.fori_loop(0, tm // ROW_BLOCK, ln_body, 0)

    cv_ref[...] = jnp.dot(cs_ref[...], wout_ref[...], preferred_element_type=F32) + bout_ref[...]
    _residual_post_norm(h_ref, cv_ref, gpost_ref, hout_ref, tm)


def _mixer_a(h, prev, gpre, wglu, bglu, wdw, bdw, lng, lnb, wout, bout, gpost, *, tm, stride, n_seq):
    n, d = h.shape
    taps = wdw.shape[0]
    hist = (taps - 1) * stride
    halo = _round_up(hist, SUBLANES)
    has_prev = prev is not None
    n_tiles = n // tm
    tiles_per_seq = 1 if has_prev else n_tiles // n_seq
    tile = pl.BlockSpec((tm, d), lambda i: (i, 0))
    weights = [gpre, wglu, bglu, wdw, bdw, lng, lnb, wout, bout, gpost]
    in_specs = [tile] + ([_const_spec(prev.shape)] if has_prev else []) + [_const_spec(w.shape) for w in weights]
    if has_prev:
        state_shape = jax.ShapeDtypeStruct((hist, d), F32)
        state_spec = pl.BlockSpec((hist, d), lambda i: (0, 0))
    else:
        state_shape = jax.ShapeDtypeStruct((n_seq, hist, d), F32)
        state_spec = pl.BlockSpec((1, hist, d), lambda i: (i // tiles_per_seq, 0, 0))
    kern = functools.partial(_mixer_a_kernel, tm=tm, stride=stride, halo=halo, taps=taps,
                             tiles_per_seq=tiles_per_seq, has_prev=has_prev)
    return pl.pallas_call(
        kern,
        grid=(n_tiles,),
        in_specs=in_specs,
        out_specs=[tile, state_spec],
        out_shape=[jax.ShapeDtypeStruct((n, d), F32), state_shape],
        scratch_shapes=[pltpu.VMEM((tm, d), BF16), pltpu.VMEM((halo + tm, d), F32),
                        pltpu.VMEM((tm, d), F32), pltpu.VMEM((tm, d), BF16)],
        compiler_params=pltpu.CompilerParams(dimension_semantics=("arbitrary",),
                                             vmem_limit_bytes=VMEM_LIMIT_BYTES),
        name="mixer_a_sample" if has_prev else "mixer_a_prompt",
    )(h, *([prev] if has_prev else []), *weights)


def _gelu_tanh(x):
    return 0.5 * x * (1.0 + jnp.tanh(0.7978845608028654 * (x + 0.044715 * (x * x * x))))


def _ffn_kernel(*refs, tm, stride, halo, tiles_per_seq, has_prev):
    if has_prev:
        (h_ref, prev_ref, gpre_ref, wg_ref, wu_ref, wdw_ref, wd_ref, gpost_ref, hout_ref, state_ref,
         xs_ref, gext_ref, up_ref, ys_ref, acc_ref) = refs
    else:
        (h_ref, gpre_ref, wg_ref, wu_ref, wdw_ref, wd_ref, gpost_ref, hout_ref, state_ref,
         xs_ref, gext_ref, up_ref, ys_ref, acc_ref) = refs
    dff = wg_ref.shape[1]
    taps = wdw_ref.shape[0]
    hist = (taps - 1) * stride

    if has_prev:
        gext_ref[halo - hist:halo, :] = prev_ref[...]
    else:
        @pl.when(pl.program_id(0) % tiles_per_seq == 0)
        def _():
            gext_ref[0:halo, :] = jnp.zeros((halo, dff), F32)

    _norm_to_bf16(h_ref, gpre_ref, xs_ref, tm)

    rb = min(ROW_BLOCK, tm)
    for c in range(dff // COL_CHUNK):
        lo, hi = c * COL_CHUNK, (c + 1) * COL_CHUNK
        gext_ref[halo:halo + tm, lo:hi] = jnp.dot(xs_ref[...], wg_ref[:, lo:hi], preferred_element_type=F32)
        up_ref[...] = jnp.dot(xs_ref[...], wu_ref[:, lo:hi], preferred_element_type=F32)

        def gate_body(r, _, lo=lo, hi=hi):
            base = pl.multiple_of(r * rb, rb)
            win = gext_ref[pl.ds(base, rb + halo), lo:hi]
            gc = None
            for k in range(taps):
                o = halo - (taps - 1 - k) * stride
                term = wdw_ref[k:k + 1, lo:hi] * win[o:o + rb]
                gc = term if gc is None else gc + term
            ys_ref[pl.ds(base, rb), :] = (_gelu_tanh(gc) * up_ref[pl.ds(base, rb), :]).astype(BF16)
            return 0
        lax.fori_loop(0, tm // rb, gate_body, 0)

        down = jnp.dot(ys_ref[...], wd_ref[lo:hi, :], preferred_element_type=F32)
        if c == 0:
            acc_ref[...] = down
        else:
            acc_ref[...] += down

    if has_prev:
        state_ref[...] = gext_ref[halo + tm - hist:halo + tm, :]
    else:
        state_ref[0] = gext_ref[halo + tm - hist:halo + tm, :]
        gext_ref[0:halo, :] = gext_ref[tm:tm + halo, :]

    _residual_post_norm(h_ref, acc_ref, gpost_ref, hout_ref, tm)


def _ffn(h, prev, gpre, wg, wu, wdw, wd, gpost, *, tm, stride, n_seq):
    n, d = h.shape
    dff = wg.shape[1]
    taps = wdw.shape[0]
    hist = (taps - 1) * stride
    halo = _round_up(hist, SUBLANES)
    has_prev = prev is not None
    n_tiles = n // tm
    tiles_per_seq = 1 if has_prev else n_tiles // n_seq
    tile = pl.BlockSpec((tm, d), lambda i: (i, 0))
    weights = [gpre, wg, wu, wdw, wd, gpost]
    in_specs = [tile] + ([_const_spec(prev.shape)] if has_prev else []) + [_const_spec(w.shape) for w in weights]
    if has_prev:
        state_shape = jax.ShapeDtypeStruct((hist, dff), F32)
        state_spec = pl.BlockSpec((hist, dff), lambda i: (0, 0))
    else:
        state_shape = jax.ShapeDtypeStruct((n_seq, hist, dff), F32)
        state_spec = pl.BlockSpec((1, hist, dff), lambda i: (i // tiles_per_seq, 0, 0))
    kern = functools.partial(_ffn_kernel, tm=tm, stride=stride, halo=halo,
                             tiles_per_seq=tiles_per_seq, has_prev=has_prev)
    return pl.pallas_call(
        kern,
        grid=(n_tiles,),
        in_specs=in_specs,
        out_specs=[tile, state_spec],
        out_shape=[jax.ShapeDtypeStruct((n, d), F32), state_shape],
        scratch_shapes=[pltpu.VMEM((tm, d), BF16), pltpu.VMEM((halo + tm, dff), F32),
                        pltpu.VMEM((tm, COL_CHUNK), F32), pltpu.VMEM((tm, COL_CHUNK), BF16),
                        pltpu.VMEM((tm, d), F32)],
        compiler_params=pltpu.CompilerParams(dimension_semantics=("arbitrary",),
                                             vmem_limit_bytes=VMEM_LIMIT_BYTES),
        name="ffn_sample" if has_prev else "ffn_prompt",
    )(h, *([prev] if has_prev else []), *weights)


def _proj_kernel(*refs, n_proj, scales):
    h_ref = refs[0]
    gain_refs = refs[1:1 + n_proj]
    w_refs = refs[1 + n_proj:1 + 2 * n_proj]
    out_refs = refs[1 + 2 * n_proj:1 + 3 * n_proj]
    xs_ref = refs[1 + 3 * n_proj]
    tm = h_ref.shape[0]
    for p in range(n_proj):
        _norm_to_bf16(h_ref, gain_refs[p], xs_ref, tm)
        y = jnp.dot(xs_ref[...], w_refs[p][...], preferred_element_type=F32)
        if scales[p] != 1.0:
            y = y * scales[p]
        out_refs[p][...] = y.astype(out_refs[p].dtype)


def _proj(h, gains, ws, scales, dtypes, *, tm):
    n, d = h.shape
    n_proj = len(ws)
    tile = pl.BlockSpec((tm, d), lambda i: (i, 0))
    out_specs = [pl.BlockSpec((tm, w.shape[1]), lambda i: (i, 0)) for w in ws]
    out_shape = [jax.ShapeDtypeStruct((n, w.shape[1]), dt) for w, dt in zip(ws, dtypes)]
    return pl.pallas_call(
        functools.partial(_proj_kernel, n_proj=n_proj, scales=tuple(scales)),
        grid=(n // tm,),
        in_specs=[tile] + [_const_spec(g.shape) for g in gains] + [_const_spec(w.shape) for w in ws],
        out_specs=out_specs,
        out_shape=out_shape,
        scratch_shapes=[pltpu.VMEM((tm, d), BF16)],
        compiler_params=pltpu.CompilerParams(dimension_semantics=("arbitrary",),
                                             vmem_limit_bytes=VMEM_LIMIT_BYTES),
        name="proj",
    )(h, *gains, *ws)


def _out_proj_kernel(h_ref, o_ref, wo_ref, gpost_ref, hout_ref, m_ref):
    m_ref[...] = jnp.dot(o_ref[...].astype(BF16), wo_ref[...], preferred_element_type=F32)
    _residual_post_norm(h_ref, m_ref, gpost_ref, hout_ref, h_ref.shape[0])


def _out_proj(h, o, wo, gpost, *, tm):
    n, d = h.shape
    tile = pl.BlockSpec((tm, d), lambda i: (i, 0))
    return pl.pallas_call(
        _out_proj_kernel,
        grid=(n // tm,),
        in_specs=[tile, pl.BlockSpec((tm, o.shape[1]), lambda i: (i, 0)),
                  _const_spec(wo.shape), _const_spec(gpost.shape)],
        out_specs=tile,
        out_shape=jax.ShapeDtypeStruct((n, d), F32),
        scratch_shapes=[pltpu.VMEM((tm, d), F32)],
        compiler_params=pltpu.CompilerParams(dimension_semantics=("arbitrary",),
                                             vmem_limit_bytes=VMEM_LIMIT_BYTES),
        name="out_proj",
    )(h, o, wo, gpost)


def _suffix_matrix(n):
    j = lax.broadcasted_iota(jnp.int32, (n, n), 0)
    s = lax.broadcasted_iota(jnp.int32, (n, n), 1)
    return jnp.where(j > s, 1.0, 0.0).astype(BF16)


def _stick_block(z, carry, tri, mask):
    softplus = jnp.maximum(z, 0.0) + jnp.log(1.0 + jnp.exp(-jnp.abs(z)))
    log_not = -softplus
    log_beta = z - softplus
    if mask is not None:
        log_not = jnp.where(mask, log_not, 0.0)
    after = jnp.dot(log_not.astype(BF16), tri, preferred_element_type=F32) + carry
    w = jnp.exp(log_beta + after)
    if mask is not None:
        w = jnp.where(mask, w, 0.0)
    new_carry = carry + jnp.sum(log_not, axis=-1, keepdims=True)
    return w, new_carry


def _attn_prompt_kernel(bias_ref, q_ref, k_ref, v_ref, o_ref, kb_ref, vb_ref, *, tq, tk):
    hp = pl.program_id(1)
    i = pl.program_id(2)

    @pl.when(i == 0)
    def _():
        kb_ref[...] = k_ref[...].astype(BF16)
        vb_ref[...] = v_ref[...].astype(BF16)

    q = q_ref[...].astype(F32)
    lane = lax.broadcasted_iota(jnp.int32, (tq, LANES), 1)
    tri = _suffix_matrix(tk)
    row = lax.broadcasted_iota(jnp.int32, (tq, tk), 0)
    col = lax.broadcasted_iota(jnp.int32, (tq, tk), 1)
    diag_mask = col < row

    outs = []
    for hh in range(HEADS_PER_LANE_TILE):
        in_head = (lane >= hh * HEAD_DIM) & (lane < (hh + 1) * HEAD_DIM)
        qh = jnp.where(in_head, q, 0.0).astype(BF16)
        bias = bias_ref[hp * HEADS_PER_LANE_TILE + hh]

        def block(kbi, acc, carry, mask, qh=qh, bias=bias):
            rows = pl.ds(pl.multiple_of(kbi * tk, tk), tk)
            z = lax.dot_general(qh, kb_ref[rows, :], (((1,), (1,)), ((), ())),
                                preferred_element_type=F32) + bias
            w, carry = _stick_block(z, carry, tri, mask)
            acc = acc + jnp.dot(w.astype(BF16), vb_ref[rows, :], preferred_element_type=F32)
            return acc, carry

        acc, carry = block(i, jnp.zeros((tq, LANES), F32), jnp.zeros((tq, 1), F32), diag_mask)

        def body(j, state):
            return block(i - j, state[0], state[1], None)
        acc, carry = lax.fori_loop(1, i + 1, body, (acc, carry))
        outs.append(acc)

    o_ref[...] = jnp.where(lane < HEAD_DIM, outs[0], outs[1]).astype(o_ref.dtype)


def _attn_prompt(q, k, v, bias, *, n_seq, t):
    n, hd = q.shape
    tq, tk = ATTN_TQ, ATTN_TK
    nq = t // tq
    kern = functools.partial(_attn_prompt_kernel, tq=tq, tk=tk)
    return pl.pallas_call(
        kern,
        grid=(n_seq, hd // LANES, nq),
        in_specs=[pl.BlockSpec(memory_space=pltpu.SMEM),
                  pl.BlockSpec((tq, LANES), lambda b, hp, i: (b * nq + i, hp)),
                  pl.BlockSpec((t, LANES), lambda b, hp, i: (b, hp)),
                  pl.BlockSpec((t, LANES), lambda b, hp, i: (b, hp))],
        out_specs=pl.BlockSpec((tq, LANES), lambda b, hp, i: (b * nq + i, hp)),
        out_shape=jax.ShapeDtypeStruct((n, hd), BF16),
        scratch_shapes=[pltpu.VMEM((t, LANES), BF16), pltpu.VMEM((t, LANES), BF16)],
        compiler_params=pltpu.CompilerParams(dimension_semantics=("arbitrary", "arbitrary", "arbitrary"),
                                             vmem_limit_bytes=VMEM_LIMIT_BYTES),
        name="attn_prompt",
    )(bias, q, k, v)


def _attn_decode_kernel(pt_ref, q_ref, brow_ref, knew_ref, vnew_ref, *rest, n_heads, n_new, page, pages):
    k_refs = rest[:pages]
    v_refs = rest[pages:2 * pages]
    o_ref = rest[2 * pages]
    qrows_ref, acc_ref, carry_ref = rest[2 * pages + 1:]
    j = pl.program_id(1)
    nrow = n_new * n_heads
    hd = n_heads * HEAD_DIM
    tri = _suffix_matrix(page)

    def tile(k_bf, v_bf, mask):
        z = lax.dot_general(qrows_ref[...], k_bf, (((1,), (1,)), ((), ())),
                            preferred_element_type=F32) + brow_ref[...]
        w, carry = _stick_block(z, carry_ref[...], tri, mask)
        carry_ref[...] = carry
        acc_ref[...] += jnp.dot(w.astype(BF16), v_bf, preferred_element_type=F32)

    @pl.when(j == 0)
    def _():
        head_of_lane = lax.broadcasted_iota(jnp.int32, (n_heads, hd), 1) // HEAD_DIM
        head_of_row = lax.broadcasted_iota(jnp.int32, (n_heads, hd), 0)
        for tt in range(n_new):
            qt = jnp.broadcast_to(q_ref[0, tt:tt + 1, :], (n_heads, hd))
            qrows_ref[tt * n_heads:(tt + 1) * n_heads, :] = jnp.where(
                head_of_lane == head_of_row, qt, 0.0).astype(BF16)
        acc_ref[...] = jnp.zeros_like(acc_ref)
        carry_ref[...] = jnp.zeros_like(carry_ref)
        pad = jnp.zeros((page - knew_ref.shape[1], hd), F32)
        k_new = jnp.concatenate([knew_ref[0], pad], axis=0).astype(BF16)
        v_new = jnp.concatenate([vnew_ref[0], pad], axis=0).astype(BF16)
        key_pos = lax.broadcasted_iota(jnp.int32, (nrow, page), 1)
        q_pos = lax.broadcasted_iota(jnp.int32, (nrow, page), 0) // n_heads
        tile(k_new, v_new, key_pos < q_pos)

    for p in reversed(range(pages)):
        tile(k_refs[p][0].astype(BF16), v_refs[p][0].astype(BF16), None)

    @pl.when(j == pl.num_programs(1) - 1)
    def _():
        head_of_lane = lax.broadcasted_iota(jnp.int32, (n_heads, hd), 1) // HEAD_DIM
        head_of_row = lax.broadcasted_iota(jnp.int32, (n_heads, hd), 0)
        for tt in range(n_new):
            blk = acc_ref[tt * n_heads:(tt + 1) * n_heads, :]
            picked = jnp.where(head_of_lane == head_of_row, blk, 0.0)
            o_ref[0, tt:tt + 1, :] = jnp.sum(picked, axis=0, keepdims=True).astype(o_ref.dtype)


def _attn_decode(q, k_new, v_new, cache_k, cache_v, page_table, bias_rows, *, n_heads):
    b, n_new, hd = q.shape
    page = cache_k.shape[1]
    n_pages = page_table.shape[1]
    pages = PAGES_PER_STEP
    while n_pages % pages:
        pages //= 2
    n_steps = n_pages // pages
    nrow = n_new * n_heads

    def page_spec(p):
        return pl.BlockSpec((1, page, hd),
                            lambda bi, j, pt: (pt[bi, (n_steps - 1 - j) * pages + p], 0, 0))

    per_seq = lambda rows: pl.BlockSpec((1, rows, hd), lambda bi, j, pt: (bi, 0, 0))
    kern = functools.partial(_attn_decode_kernel, n_heads=n_heads, n_new=n_new, page=page, pages=pages)
    grid_spec = pltpu.PrefetchScalarGridSpec(
        num_scalar_prefetch=1,
        grid=(b, n_steps),
        in_specs=[per_seq(n_new),
                  pl.BlockSpec((nrow, 1), lambda bi, j, pt: (0, 0)),
                  per_seq(k_new.shape[1]), per_seq(v_new.shape[1])]
                 + [page_spec(p) for p in range(pages)] + [page_spec(p) for p in range(pages)],
        out_specs=per_seq(n_new),
        scratch_shapes=[pltpu.VMEM((nrow, hd), BF16), pltpu.VMEM((nrow, hd), F32),
                        pltpu.VMEM((nrow, 1), F32)],
    )
    return pl.pallas_call(
        kern,
        grid_spec=grid_spec,
        out_shape=jax.ShapeDtypeStruct((b, n_new, hd), F32),
        compiler_params=pltpu.CompilerParams(dimension_semantics=("arbitrary", "arbitrary"),
                                             vmem_limit_bytes=VMEM_LIMIT_BYTES),
        name="attn_decode",
    )(page_table, q, bias_rows, k_new, v_new, *([cache_k] * pages), *([cache_v] * pages))


def _trunk(h, p, *, n_seq, t, tm, stride, conv_prev, ffn_prev, attend, q_dtype):
    depth = p["norm_mix_pre"].shape[0]
    n_a = p["conv_w_glu"].shape[0]
    row = lambda v: v.reshape(1, -1)
    conv_states, ffn_states = [], []
    k_new = v_new = None
    for layer in range(depth):
        gpre, gpost = row(p["norm_mix_pre"][layer]), row(p["norm_mix_post"][layer])
        if layer < n_a:
            h, st = _mixer_a(
                h, None if conv_prev is None else conv_prev[layer], gpre,
                p["conv_w_glu"][layer], row(p["conv_b_glu"][layer]), p["conv_w_dw"][layer],
                row(p["conv_b_dw"][layer]), row(p["conv_ln_g"][layer]), row(p["conv_ln_b"][layer]),
                p["conv_w_out"][layer], row(p["conv_b_out"][layer]), gpost,
                tm=tm, stride=stride, n_seq=n_seq)
            conv_states.append(st)
        else:
            jb = layer - n_a
            scale = HEAD_DIM ** -0.5
            if layer == n_a:
                q, k_new, v_new = _proj(
                    h, [gpre, row(p["kv_norm"]), row(p["kv_norm"])], [p["w_q"][jb], p["w_k"], p["w_v"]],
                    [scale, 1.0, 1.0], [q_dtype, F32, F32], tm=tm)
            else:
                (q,) = _proj(h, [gpre], [p["w_q"][jb]], [scale], [q_dtype], tm=tm)
            o = attend(q, k_new, v_new, p["sb_bias"][jb])
            h = _out_proj(h, o, p["w_o"][jb], gpost, tm=tm)
        h, st_f = _ffn(
            h, None if ffn_prev is None else ffn_prev[layer], row(p["norm_ffn_pre"][layer]),
            p["ffn_w_gate"][layer], p["ffn_w_up"][layer], p["ffn_w_dw"][layer], p["ffn_w_down"][layer],
            row(p["norm_ffn_post"][layer]), tm=tm, stride=stride, n_seq=n_seq)
        ffn_states.append(st_f)
    return h, conv_states, ffn_states, k_new, v_new


def kernel(x_prompt, x_sample, state_conv_a, state_ffn_conv, cache_k, cache_v, page_table, norm_mix_pre, norm_mix_post, norm_ffn_pre, norm_ffn_post, conv_w_glu, conv_b_glu, conv_w_dw, conv_b_dw, conv_ln_g, conv_ln_b, conv_w_out, conv_b_out, kv_norm, w_k, w_v, w_q, w_o, sb_bias, ffn_w_gate, ffn_w_up, ffn_w_dw, ffn_w_down):
    bp, t, d = x_prompt.shape
    bs, ts, _ = x_sample.shape
    n_heads = d // HEAD_DIM
    dff = ffn_w_gate.shape[2]
    n_a = conv_w_glu.shape[0]
    depth = norm_mix_pre.shape[0]
    hist_a = conv_w_dw.shape[1] - 1
    hist_f = ffn_w_dw.shape[1] - 1
    assert d % COL_CHUNK == 0 and dff % COL_CHUNK == 0 and bs % SUBLANES == 0
    assert ts >= hist_f and t % ATTN_TQ == 0 and ATTN_TQ == ATTN_TK

    p = dict(norm_mix_pre=norm_mix_pre, norm_mix_post=norm_mix_post, norm_ffn_pre=norm_ffn_pre,
             norm_ffn_post=norm_ffn_post, conv_b_glu=conv_b_glu, conv_w_dw=conv_w_dw, conv_b_dw=conv_b_dw,
             conv_ln_g=conv_ln_g, conv_ln_b=conv_ln_b, conv_b_out=conv_b_out, kv_norm=kv_norm,
             sb_bias=sb_bias, ffn_w_dw=ffn_w_dw,
             conv_w_glu=conv_w_glu.astype(BF16), conv_w_out=conv_w_out.astype(BF16),
             w_k=w_k.astype(BF16), w_v=w_v.astype(BF16), w_q=w_q.astype(BF16), w_o=w_o.astype(BF16),
             ffn_w_gate=ffn_w_gate.astype(BF16), ffn_w_up=ffn_w_up.astype(BF16),
             ffn_w_down=ffn_w_down.astype(BF16))

    tm_p = min(PROMPT_TILE, t)
    attend_p = lambda q, k, v, bias: _attn_prompt(q, k, v, bias, n_seq=bp, t=t)
    hp_, conv_p, ffn_p, k_p, v_p = _trunk(
        x_prompt.reshape(bp * t, d), p, n_seq=bp, t=t, tm=tm_p, stride=1,
        conv_prev=None, ffn_prev=None, attend=attend_p, q_dtype=BF16)

    n_s = bs * ts
    conv_prev_s = state_conv_a.transpose(0, 2, 1, 3).reshape(n_a, hist_a * bs, d)
    ffn_prev_s = state_ffn_conv.transpose(0, 2, 1, 3).reshape(depth, hist_f * bs, dff)
    cache_k2 = cache_k.reshape(cache_k.shape[0], cache_k.shape[1], n_heads * HEAD_DIM)
    cache_v2 = cache_v.reshape(cache_v.shape[0], cache_v.shape[1], n_heads * HEAD_DIM)

    def to_seq_major(x2d):
        return x2d.reshape(ts, bs, -1).transpose(1, 0, 2)

    def attend_s(q, k, v, bias):
        pad = ((0, 0), (0, SUBLANES - ts), (0, 0))
        bias_rows = jnp.tile(bias.astype(F32), ts).reshape(ts * n_heads, 1)
        o = _attn_decode(to_seq_major(q), jnp.pad(to_seq_major(k), pad), jnp.pad(to_seq_major(v), pad),
                         cache_k2, cache_v2, page_table, bias_rows, n_heads=n_heads)
        return o.transpose(1, 0, 2).reshape(n_s, -1)

    hs_, conv_s, ffn_s, k_s, v_s = _trunk(
        x_sample.transpose(1, 0, 2).reshape(n_s, d), p, n_seq=bs, t=ts, tm=n_s, stride=bs,
        conv_prev=conv_prev_s, ffn_prev=ffn_prev_s, attend=attend_s, q_dtype=F32)

    def state_seq_major(st, hist):
        return st.reshape(hist, bs, -1).transpose(1, 0, 2)

    return (hp_.reshape(bp, t, d),
            to_seq_major(hs_),
            jnp.stack(conv_p),
            jnp.stack([state_seq_major(s, hist_a) for s in conv_s]),
            jnp.stack(ffn_p),
            jnp.stack([state_seq_major(s, hist_f) for s in ffn_s]),
            k_p.reshape(bp, t, n_heads, HEAD_DIM), v_p.reshape(bp, t, n_heads, HEAD_DIM),
            to_seq_major(k_s).reshape(bs, ts, n_heads, HEAD_DIM),
            to_seq_major(v_s).reshape(bs, ts, n_heads, HEAD_DIM))
```
